```python
import jax, jax.numpy as jnp
from jax import lax
import numpy as np

D_MODEL = 4096
BATCH = 1
SEQ = 8192
DEPTH = 4

HEAD_DIM = 128
MIX_WIDTH = D_MODEL
SB_WIDTH = MIX_WIDTH // 2
SB_HEADS = SB_WIDTH // HEAD_DIM
HG_WIDTH = MIX_WIDTH - SB_WIDTH
HG_HEADS = HG_WIDTH // HEAD_DIM
HG_KDIM = HEAD_DIM
HG_VDIM = HEAD_DIM
IN_COLS = 3 * SB_WIDTH + 4 * HG_WIDTH
D_FF = 256 * ((8 * D_MODEL // 3 + 255) // 256)
CONV_WIDTH = 3
SB_BLOCK = 128
HG_CHUNK = 64
EPS = 1e-6

kernel_name = "hybrid_stickbreak_hgrn2_convffn"


def rmsnorm(x, w):
    xf = x.astype(jnp.float32)
    y = xf * lax.rsqrt(jnp.mean(xf * xf, axis=-1, keepdims=True) + EPS)
    return (y * w.astype(jnp.float32)).astype(x.dtype)


def head_rmsnorm(o, w, n_heads):
    B, H, T, dh = o.shape
    o = jnp.transpose(o, (0, 2, 1, 3))
    o = o * lax.rsqrt(jnp.mean(o * o, axis=-1, keepdims=True) + EPS)
    o = o * w.astype(jnp.float32).reshape(n_heads, dh)
    return o.reshape(B, T, H * dh)


def split_heads(a, n_heads):
    B, T, C = a.shape
    return jnp.transpose(a.reshape(B, T, n_heads, C // n_heads), (0, 2, 1, 3))


def stick_breaking_attention(q, k, v):
    T = q.shape[2]
    scale = q.shape[-1] ** -0.5
    outs = []
    for blk in range(T // SB_BLOCK):
        start = blk * SB_BLOCK
        end = start + SB_BLOCK
        qb = q[:, :, start:end]
        kb = k[:, :, :end]
        vb = v[:, :, :end]
        z = jnp.einsum('bhqd,bhkd->bhqk', qb, kb) * scale
        qpos = start + jnp.arange(SB_BLOCK)[:, None]
        kpos = jnp.arange(end)[None, :]
        strict = kpos < qpos
        log_keep = jnp.where(strict, jax.nn.log_sigmoid(-z), 0.0)
        log_remain = lax.cumsum(log_keep, axis=3, reverse=True) - log_keep
        log_w = jnp.where(strict, jax.nn.log_sigmoid(z) + log_remain, -jnp.inf)
        outs.append(jnp.einsum('bhqk,bhkd->bhqd', jnp.exp(log_w), vb))
    return jnp.concatenate(outs, axis=2)


def hgrn2_chunked(q, k, log_f, v):
    B, H, T, dk = q.shape
    dv = v.shape[-1]
    nc = T // HG_CHUNK

    def to_chunks(a):
        return jnp.moveaxis(a.reshape(B, H, nc, HG_CHUNK, a.shape[-1]), 2, 0)

    causal = jnp.arange(HG_CHUNK)[:, None] >= jnp.arange(HG_CHUNK)[None, :]

    def step(S, inp):
        qc, kc, gc, vc = inp
        b = jnp.cumsum(gc, axis=2)
        diff = b[:, :, :, None, :] - b[:, :, None, :, :]
        decay = jnp.exp(jnp.where(causal[:, :, None], diff, -jnp.inf))
        scores = jnp.einsum('bhtk,bhtsk,bhsk->bhts', qc, decay, kc)
        o = (jnp.einsum('bhts,bhsv->bhtv', scores, vc)
             + jnp.einsum('bhtk,bhkv->bhtv', qc * jnp.exp(b), S))
        b_last = b[:, :, -1:, :]
        S = (jnp.exp(b_last[:, :, 0, :])[..., None] * S
             + jnp.einsum('bhsk,bhsv->bhkv', kc * jnp.exp(b_last - b), vc))
        return S, o

    S0 = jnp.zeros((B, H, dk, dv), jnp.float32)
    _, o = lax.scan(step, S0, (to_chunks(q), to_chunks(k), to_chunks(log_f), to_chunks(v)))
    return jnp.moveaxis(o, 0, 2).reshape(B, H, T, dv)


def hybrid_mixer(h, w_in_l, sb_norm_l, lb_l, hg_norm_l, w_out_l):
    proj = (h @ w_in_l).astype(jnp.float32)
    cuts = np.cumsum([SB_WIDTH, SB_WIDTH, SB_WIDTH, HG_WIDTH, HG_WIDTH, HG_WIDTH])
    q_sb, k_sb, v_sb, q_hg, f_hg, i_hg, g_hg = jnp.split(proj, cuts, axis=-1)

    o_sb = stick_breaking_attention(split_heads(q_sb, SB_HEADS),
                                    split_heads(k_sb, SB_HEADS),
                                    split_heads(v_sb, SB_HEADS))
    o_sb = head_rmsnorm(o_sb, sb_norm_l, SB_HEADS)

    lb = lb_l.astype(jnp.float32)
    log_f = jnp.logaddexp(jnp.log(lb), jnp.log1p(-lb) + jax.nn.log_sigmoid(f_hg))
    k_hg = (1.0 - lb) * jax.nn.sigmoid(-f_hg)
    o_hg = hgrn2_chunked(split_heads(jax.nn.silu(q_hg), HG_HEADS),
                         split_heads(k_hg, HG_HEADS),
                         split_heads(log_f, HG_HEADS),
                         split_heads(i_hg, HG_HEADS))
    o_hg = head_rmsnorm(o_hg, hg_norm_l, HG_HEADS) * jax.nn.silu(g_hg)

    merged = jnp.concatenate([o_sb, o_hg], axis=-1).astype(h.dtype)
    return merged @ w_out_l


def conv_ffn(h, w_up_l, conv_w_l, conv_b_l, w_down_l):
    u = h @ w_up_l
    T = u.shape[1]
    u_pad = jnp.pad(u, ((0, 0), (CONV_WIDTH - 1, 0), (0, 0)))
    uc = conv_b_l
    for j in range(CONV_WIDTH):
        uc = uc + conv_w_l[j] * u_pad[:, j:j + T]
    gate, up = jnp.split(uc, 2, axis=-1)
    return (jax.nn.silu(gate) * up) @ w_down_l


def setup_inputs(seed: int = 0) -> dict:
    key = jax.random.key(seed)
    ks = jax.random.split(key, 14)
    f32 = jnp.float32
    x = jax.random.normal(ks[0], (BATCH, SEQ, D_MODEL), f32)
    norm1_w = 1.0 + 0.02 * jax.random.normal(ks[1], (DEPTH, D_MODEL), f32)
    w_in = jax.random.normal(ks[2], (DEPTH, D_MODEL, IN_COLS), f32) * D_MODEL ** -0.5
    sb_norm_w = 1.0 + 0.02 * jax.random.normal(ks[3], (DEPTH, SB_WIDTH), f32)
    hg_lb_param = jax.random.normal(ks[4], (DEPTH, HG_WIDTH), f32)
    hg_norm_w = 1.0 + 0.02 * jax.random.normal(ks[5], (DEPTH, HG_WIDTH), f32)
    w_out = jax.random.normal(ks[6], (DEPTH, MIX_WIDTH, D_MODEL), f32) * MIX_WIDTH ** -0.5
    norm2_w = 1.0 + 0.02 * jax.random.normal(ks[7], (DEPTH, D_MODEL), f32)
    w_up = jax.random.normal(ks[8], (DEPTH, D_MODEL, 2 * D_FF), f32) * D_MODEL ** -0.5
    conv_w = jax.random.normal(ks[9], (DEPTH, CONV_WIDTH, 2 * D_FF), f32) * CONV_WIDTH ** -0.5
    conv_b = 0.02 * jax.random.normal(ks[10], (DEPTH, 2 * D_FF), f32)
    w_down = jax.random.normal(ks[11], (DEPTH, D_FF, D_MODEL), f32) * D_FF ** -0.5
    final_norm_w = 1.0 + 0.02 * jax.random.normal(ks[12], (D_MODEL,), f32)
    return {"x": x, "norm1_w": norm1_w, "w_in": w_in, "sb_norm_w": sb_norm_w,
            "hg_lb_param": hg_lb_param, "hg_norm_w": hg_norm_w, "w_out": w_out,
            "norm2_w": norm2_w, "w_up": w_up, "conv_w": conv_w, "conv_b": conv_b,
            "w_down": w_down, "final_norm_w": final_norm_w}


def reference(x, norm1_w, w_in, sb_norm_w, hg_lb_param, hg_norm_w, w_out,
              norm2_w, w_up, conv_w, conv_b, w_down, final_norm_w):
    lb_all = jnp.cumsum(jax.nn.softmax(hg_lb_param.astype(jnp.float32), axis=0), axis=0)
    lb_all = lb_all - lb_all[0:1]
    h = x
    for l in range(DEPTH):
        h = h + hybrid_mixer(rmsnorm(h, norm1_w[l]), w_in[l], sb_norm_w[l], lb_all[l],
                             hg_norm_w[l], w_out[l])
        h = h + conv_ffn(rmsnorm(h, norm2_w[l]), w_up[l], conv_w[l], conv_b[l], w_down[l])
    return rmsnorm(h, final_norm_w)
```

```python
import functools

import jax
import jax.numpy as jnp
from jax import lax
from jax.experimental import pallas as pl
from jax.experimental.pallas import tpu as pltpu

F32 = jnp.float32
BF16 = jnp.bfloat16

HEAD_DIM = 128
CONV_WIDTH = 3
EPS = 1e-6

V7X_VMEM_LIMIT_BYTES = 56 * 1024 * 1024
HG_CHUNK = 64
HG_SUB = 16
HG_EXP_CLAMP = 80.0


def _params(*sem):
    return pltpu.CompilerParams(dimension_semantics=sem, vmem_limit_bytes=V7X_VMEM_LIMIT_BYTES)


def _softplus(x):
    return jnp.maximum(x, 0.0) + jnp.log1p(jnp.exp(-jnp.abs(x)))


def _sigmoid(x):
    return 1.0 / (1.0 + jnp.exp(-x))


def _rmsnorm_kernel(x_ref, w_ref, o_ref):
    x = x_ref[...]
    y = x * lax.rsqrt(jnp.mean(x * x, axis=-1, keepdims=True) + EPS)
    o_ref[...] = (y * w_ref[...]).astype(o_ref.dtype)


def rmsnorm(x, w, out_dtype, tm=256):
    t, d = x.shape
    return pl.pallas_call(
        _rmsnorm_kernel,
        grid=(t // tm,),
        in_specs=[pl.BlockSpec((tm, d), lambda i: (i, 0)),
                  pl.BlockSpec((1, d), lambda i: (0, 0))],
        out_specs=pl.BlockSpec((tm, d), lambda i: (i, 0)),
        out_shape=jax.ShapeDtypeStruct((t, d), out_dtype),
        compiler_params=_params("arbitrary"),
        name="rmsnorm",
    )(x, w.reshape(1, d))


def _mm_kernel(*refs, n_a, has_res, scale, n_scaled):
    a_refs = refs[:n_a]
    w_refs = refs[n_a:2 * n_a]
    pos = 2 * n_a
    res_ref = refs[pos] if has_res else None
    pos += int(has_res)
    o_ref = refs[pos]
    wbf_refs = refs[pos + 1:pos + 1 + n_a]
    j = pl.program_id(0)
    i = pl.program_id(1)

    @pl.when(i == 0)
    def _():
        for w_ref, wbf_ref in zip(w_refs, wbf_refs):
            wbf_ref[...] = w_ref[...].astype(BF16)

    acc = None
    for a_ref, wbf_ref in zip(a_refs, wbf_refs):
        d = jnp.dot(a_ref[...], wbf_ref[...], preferred_element_type=F32)
        acc = d if acc is None else acc + d
    if scale is not None:
        acc = acc * jnp.where(j < n_scaled, scale, 1.0).astype(F32)
    if has_res:
        acc = res_ref[...] + acc
    o_ref[...] = acc.astype(o_ref.dtype)


def matmul(a_list, w, layer, *, n_out, col_off=0, res=None, out_dtype=F32, tm, tn,
           scale=None, n_scaled_cols=0):
    m = a_list[0].shape[0]
    ks = [a.shape[1] for a in a_list]
    assert sum(ks) == w.shape[1] and len(set(ks)) == 1
    assert m % tm == 0 and n_out % tn == 0 and col_off % tn == 0
    coff = col_off // tn
    in_specs = [pl.BlockSpec((tm, k), lambda j, i: (i, 0)) for k in ks]
    in_specs += [pl.BlockSpec((None, k, tn),
                              functools.partial(lambda j, i, p: (layer, p, j + coff), p=p))
                 for p, k in enumerate(ks)]
    args = list(a_list) + [w] * len(a_list)
    if res is not None:
        in_specs.append(pl.BlockSpec((tm, tn), lambda j, i: (i, j)))
        args.append(res)
    kern = functools.partial(_mm_kernel, n_a=len(a_list), has_res=res is not None,
                             scale=scale, n_scaled=n_scaled_cols // tn)
    return pl.pallas_call(
        kern,
        grid=(n_out // tn, m // tm),
        in_specs=in_specs,
        out_specs=pl.BlockSpec((tm, tn), lambda j, i: (i, j)),
        out_shape=jax.ShapeDtypeStruct((m, n_out), out_dtype),
        scratch_shapes=[pltpu.VMEM((k, tn), BF16) for k in ks],
        compiler_params=_params("arbitrary", "arbitrary"),
        name="matmul",
    )(*args)


def _sb_kernel(q_ref, k_ref, v_ref, u_ref, nw_ref, o_ref, *, tq):
    qi = pl.program_id(1)
    q = q_ref[...]
    u = u_ref[...]

    def load(kb):
        start = pl.multiple_of(kb * tq, tq)
        k = k_ref[pl.ds(start, tq), :]
        v = v_ref[pl.ds(start, tq), :]
        z = lax.dot_general(q, k, (((1,), (1,)), ((), ())), preferred_element_type=F32)
        return z, -_softplus(z), v

    def accumulate(z, lk, v, carry, acc, mask):
        hi = lk.astype(BF16)
        lo = (lk - hi.astype(F32)).astype(BF16)
        c = (jnp.dot(hi, u, preferred_element_type=F32)
             + jnp.dot(lo, u, preferred_element_type=F32))
        w = jnp.exp(z + c + carry)
        if mask is not None:
            w = jnp.where(mask, w, 0.0)
        acc = acc + jnp.dot(w.astype(BF16), v, preferred_element_type=F32)
        carry = carry + jnp.sum(lk, axis=1, keepdims=True)
        return carry, acc

    z, lk, v = load(qi)
    row = lax.broadcasted_iota(jnp.int32, (tq, tq), 0)
    col = lax.broadcasted_iota(jnp.int32, (tq, tq), 1)
    mask = col < row
    lk = jnp.where(mask, lk, 0.0)
    carry, acc = accumulate(z, lk, v, jnp.zeros((tq, 1), F32),
                            jnp.zeros((tq, HEAD_DIM), F32), mask)

    def body(n, state):
        z, lk, v = load(qi - 1 - n)
        return accumulate(z, lk, v, state[0], state[1], None)

    carry, acc = lax.fori_loop(0, qi, body, (carry, acc))
    o = acc * lax.rsqrt(jnp.mean(acc * acc, axis=-1, keepdims=True) + EPS)
    o_ref[...] = (o * nw_ref[...]).astype(o_ref.dtype)


def sb_attention(qkv, norm_w, n_heads, tq=256):
    t = qkv.shape[0]
    dh = HEAD_DIM
    idx = jnp.arange(tq)
    u = (idx[:, None] >= idx[None, :]).astype(BF16)
    return pl.pallas_call(
        functools.partial(_sb_kernel, tq=tq),
        grid=(n_heads, t // tq),
        in_specs=[pl.BlockSpec((tq, dh), lambda h, i: (i, h)),
                  pl.BlockSpec((t, dh), lambda h, i: (0, n_heads + h)),
                  pl.BlockSpec((t, dh), lambda h, i: (0, 2 * n_heads + h)),
                  pl.BlockSpec((tq, tq), lambda h, i: (0, 0)),
                  pl.BlockSpec((1, dh), lambda h, i: (0, h))],
        out_specs=pl.BlockSpec((tq, dh), lambda h, i: (i, h)),
        out_shape=jax.ShapeDtypeStruct((t, n_heads * dh), BF16),
        compiler_params=_params("arbitrary", "arbitrary"),
        name="sb_attention",
    )(qkv, qkv, qkv, u, norm_w.reshape(1, n_heads * dh))


def _lb_kernel(p_ref, lb_ref):
    p = p_ref[...]
    depth = p.shape[0]
    e = jnp.exp(p - jnp.max(p, axis=0, keepdims=True))
    sm = e / jnp.sum(e, axis=0, keepdims=True)
    c = sm[0:1]
    first = c
    rows = [c - first]
    for l in range(1, depth):
        c = c + sm[l:l + 1]
        rows.append(c - first)
    lb_ref[...] = jnp.concatenate(rows, axis=0)


def hgrn_lower_bounds(p):
    return pl.pallas_call(
        _lb_kernel,
        out_shape=jax.ShapeDtypeStruct(p.shape, F32),
        name="hgrn_lower_bounds",
    )(p)


def _hgrn_kernel(q_ref, f_ref, i_ref, g_ref, lb_ref, nw_ref, o_ref, st_ref, *, n_heads):
    dh = HEAD_DIM
    cc = HG_CHUNK
    nsub = cc // HG_SUB

    @pl.when(pl.program_id(0) == 0)
    def _():
        st_ref[...] = jnp.zeros_like(st_ref)

    fz = f_ref[...]
    lb = lb_ref[...]
    e = jnp.exp(-jnp.abs(fz))
    log_sig = jnp.minimum(fz, 0.0) - jnp.log1p(e)
    r = 1.0 / (1.0 + e)
    sig_neg = jnp.where(fz >= 0.0, e * r, r)
    a = jnp.log(lb)
    b2 = jnp.log1p(-lb) + log_sig
    log_f = jnp.maximum(a, b2) + jnp.log1p(jnp.exp(-jnp.abs(a - b2)))
    kk = (1.0 - lb) * sig_neg
    qz = q_ref[...]
    qq = qz * _sigmoid(qz)
    gz = g_ref[...]
    gate = gz * _sigmoid(gz)

    row_w = lax.broadcasted_iota(jnp.int32, log_f.shape, 0)
    b = log_f
    s = 1
    while s < cc:
        b = b + jnp.where(row_w >= s, pltpu.roll(b, s, axis=0), 0.0)
        s *= 2

    row = lax.broadcasted_iota(jnp.int32, (cc, dh), 0)
    r2 = lax.broadcasted_iota(jnp.int32, (cc, cc), 0)
    c2 = lax.broadcasted_iota(jnp.int32, (cc, cc), 1)
    causal = c2 <= r2
    zeros_sub = jnp.zeros((HG_SUB, dh), BF16)

    for h in range(n_heads):
        sl = slice(h * dh, (h + 1) * dh)
        bh, qh, kh, vh = b[:, sl], qq[:, sl], kk[:, sl], i_ref[:, sl]
        vb = vh.astype(BF16)
        q_parts, k_rows = [], []
        for jb in range(nsub):
            anchor = bh[jb * HG_SUB:jb * HG_SUB + 1]
            qd = qh * jnp.exp(jnp.minimum(bh - anchor, 0.0))
            q_parts.append(jnp.where(row >= jb * HG_SUB, qd, 0.0).astype(BF16))
            rows = slice(jb * HG_SUB, (jb + 1) * HG_SUB)
            ks = kh[rows] * jnp.exp(jnp.minimum(anchor - bh[rows], HG_EXP_CLAMP))
            k_rows.append(jnp.concatenate(
                [zeros_sub] * jb + [ks.astype(BF16)] + [zeros_sub] * (nsub - 1 - jb), axis=1))
        q_cat = jnp.concatenate(q_parts, axis=1)
        k_cat = jnp.concatenate(k_rows, axis=0)
        scores = lax.dot_general(q_cat, k_cat, (((1,), (1,)), ((), ())),
                                 preferred_element_type=F32)
        p = jnp.where(causal, scores, 0.0).astype(BF16)
        o = jnp.dot(p, vb, preferred_element_type=F32)
        st = st_ref[h]
        q0 = (qh * jnp.exp(bh)).astype(BF16)
        o = o + lax.dot_general(q0, st.astype(BF16), (((1,), (1,)), ((), ())),
                                preferred_element_type=F32)
        b_last = bh[cc - 1:cc]
        k_end = (kh * jnp.exp(b_last - bh)).astype(BF16)
        st_ref[h] = st * jnp.exp(b_last) + lax.dot_general(
            vb, k_end, (((0,), (0,)), ((), ())), preferred_element_type=F32)
        o = o * lax.rsqrt(jnp.mean(o * o, axis=-1, keepdims=True) + EPS)
        o_ref[:, sl] = (o * nw_ref[:, sl] * gate[:, sl]).astype(o_ref.dtype)


def hgrn2(hg, lb, norm_w, n_heads):
    t = hg.shape[0]
    w = n_heads * HEAD_DIM
    cc = HG_CHUNK
    col = lambda c: pl.BlockSpec((cc, w), lambda i: (i, c))
    vec = pl.BlockSpec((1, w), lambda i: (0, 0))
    return pl.pallas_call(
        functools.partial(_hgrn_kernel, n_heads=n_heads),
        grid=(t // cc,),
        in_specs=[col(0), col(1), col(2), col(3), vec, vec],
        out_specs=pl.BlockSpec((cc, w), lambda i: (i, 0)),
        out_shape=jax.ShapeDtypeStruct((t, w), BF16),
        scratch_shapes=[pltpu.VMEM((n_heads, HEAD_DIM, HEAD_DIM), F32)],
        compiler_params=_params("arbitrary"),
        name="hgrn2",
    )(hg, hg, hg, hg, lb.reshape(1, w), norm_w.reshape(1, w))


def _up_kernel(a_ref, wg_ref, wu_ref, cwg_ref, cwu_ref, cbg_ref, cbu_ref, o_ref,
               wg_bf, wu_bf, tail_g, tail_u, *, tm, sub):
    i = pl.program_id(1)
    tn = o_ref.shape[1]

    @pl.when(i == 0)
    def _():
        wg_bf[...] = wg_ref[...].astype(BF16)
        wu_bf[...] = wu_ref[...].astype(BF16)
        tail_g[...] = jnp.zeros_like(tail_g)
        tail_u[...] = jnp.zeros_like(tail_u)

    row = lax.broadcasted_iota(jnp.int32, (sub, tn), 0)

    def conv(x, tail, cw_ref, cb_ref):
        m2, m1 = tail[0:1], tail[1:2]
        x1 = jnp.where(row == 0, m1, pltpu.roll(x, 1, axis=0))
        x2 = jnp.where(row == 0, m2, jnp.where(row == 1, m1, pltpu.roll(x, 2, axis=0)))
        y = cb_ref[...] + cw_ref[0:1] * x2
        y = y + cw_ref[1:2] * x1
        y = y + cw_ref[2:3] * x
        return y, x[sub - 2:sub]

    tg, tu = tail_g[0:2], tail_u[0:2]
    for r in range(tm // sub):
        a = a_ref[r * sub:(r + 1) * sub, :]
        g = jnp.dot(a, wg_bf[...], preferred_element_type=F32)
        u = jnp.dot(a, wu_bf[...], preferred_element_type=F32)
        gc, tg = conv(g, tg, cwg_ref, cbg_ref)
        uc, tu = conv(u, tu, cwu_ref, cbu_ref)
        o_ref[r * sub:(r + 1) * sub, :] = (gc * _sigmoid(gc) * uc).astype(o_ref.dtype)
    tail_g[0:2] = tg
    tail_u[0:2] = tu


def conv_gated_up(a, w_up, layer, conv_w, conv_b, *, tm, tn, sub=256):
    m, k = a.shape
    d_ff = w_up.shape[2] // 2
    assert d_ff % tn == 0 and m % tm == 0 and tm % sub == 0
    nj = d_ff // tn
    cb = conv_b.reshape(1, 2 * d_ff)
    return pl.pallas_call(
        functools.partial(_up_kernel, tm=tm, sub=sub),
        grid=(nj, m // tm),
        in_specs=[pl.BlockSpec((tm, k), lambda j, i: (i, 0)),
                  pl.BlockSpec((None, k, tn), lambda j, i: (layer, 0, j)),
                  pl.BlockSpec((None, k, tn), lambda j, i: (layer, 0, j + nj)),
                  pl.BlockSpec((CONV_WIDTH, tn), lambda j, i: (0, j)),
                  pl.BlockSpec((CONV_WIDTH, tn), lambda j, i: (0, j + nj)),
                  pl.BlockSpec((1, tn), lambda j, i: (0, j)),
                  pl.BlockSpec((1, tn), lambda j, i: (0, j + nj))],
        out_specs=pl.BlockSpec((tm, tn), lambda j, i: (i, j)),
        out_shape=jax.ShapeDtypeStruct((m, d_ff), BF16),
        scratch_shapes=[pltpu.VMEM((k, tn), BF16), pltpu.VMEM((k, tn), BF16),
                        pltpu.VMEM((8, tn), F32), pltpu.VMEM((8, tn), F32)],
        compiler_params=_params("arbitrary", "arbitrary"),
        name="conv_gated_up",
    )(a, w_up, w_up, conv_w, conv_w, cb, cb)


def kernel(x, norm1_w, w_in, sb_norm_w, hg_lb_param, hg_norm_w, w_out, norm2_w, w_up, conv_w,
           conv_b, w_down, final_norm_w):
    batch, seq, d_model = x.shape
    assert batch == 1
    depth = w_in.shape[0]
    sb_width = sb_norm_w.shape[1]
    hg_width = hg_norm_w.shape[1]
    sb_heads = sb_width // HEAD_DIM
    hg_heads = hg_width // HEAD_DIM
    scale = HEAD_DIM ** -0.5

    lb_all = hgrn_lower_bounds(hg_lb_param)
    h = x.reshape(seq, d_model)
    for l in range(depth):
        hn = rmsnorm(h, norm1_w[l], BF16)
        qkv = matmul([hn], w_in, l, n_out=3 * sb_width, out_dtype=BF16, tm=1024, tn=512,
                     scale=scale, n_scaled_cols=sb_width)
        hg = matmul([hn], w_in, l, n_out=4 * hg_width, col_off=3 * sb_width, tm=1024, tn=512)
        o_sb = sb_attention(qkv, sb_norm_w[l], sb_heads)
        o_hg = hgrn2(hg, lb_all[l], hg_norm_w[l], hg_heads)
        h = matmul([o_sb, o_hg], w_out, l, n_out=d_model, res=h, tm=1024, tn=512)
        hn = rmsnorm(h, norm2_w[l], BF16)
        act = conv_gated_up(hn, w_up, l, conv_w[l], conv_b[l], tm=1024, tn=256)
        h = matmul([act], w_down, l, n_out=d_model, res=h, tm=512, tn=256)
    return rmsnorm(h, final_norm_w, F32).reshape(batch, seq, d_model)
```

```python
import functools

import jax
import jax.numpy as jnp
from jax import lax
from jax.experimental import pallas as pl
from jax.experimental.pallas import tpu as pltpu

F32 = jnp.float32
BF16 = jnp.bfloat16

HEAD_DIM = 128
SUBLANES = 8
CONV_WIDTH = 3
EPS = 1e-6
LOG2E = 1.4426950408889634

V7X_VMEM_LIMIT_BYTES = 56 * 1024 * 1024
HG_CHUNK = 64
HG_SUB = 16
HG_EXP_CLAMP = 80.0


def _params(*sem, flags=None):
    return pltpu.CompilerParams(dimension_semantics=sem, vmem_limit_bytes=V7X_VMEM_LIMIT_BYTES,
                                flags=flags)


def _softplus(x):
    return jnp.maximum(x, 0.0) + jnp.log1p(jnp.exp(-jnp.abs(x)))


def _sigmoid(x):
    return 1.0 / (1.0 + jnp.exp(-x))


def _rmsnorm_kernel(x_ref, w_ref, o_ref):
    x = x_ref[...]
    y = x * lax.rsqrt(jnp.mean(x * x, axis=-1, keepdims=True) + EPS)
    o_ref[...] = (y * w_ref[...]).astype(o_ref.dtype)


def rmsnorm(x, w, out_dtype, tm=256):
    t, d = x.shape
    return pl.pallas_call(
        _rmsnorm_kernel,
        grid=(t // tm,),
        in_specs=[pl.BlockSpec((tm, d), lambda i: (i, 0)),
                  pl.BlockSpec((1, d), lambda i: (0, 0))],
        out_specs=pl.BlockSpec((tm, d), lambda i: (i, 0)),
        out_shape=jax.ShapeDtypeStruct((t, d), out_dtype),
        compiler_params=_params("arbitrary"),
        name="rmsnorm",
    )(x, w.reshape(1, d))


def _mm_kernel(*refs, n_a, has_res, cast_w, scale, n_scaled):
    a_refs = refs[:n_a]
    w_refs = refs[n_a:2 * n_a]
    pos = 2 * n_a
    res_ref = refs[pos] if has_res else None
    pos += int(has_res)
    o_ref = refs[pos]
    wbf_refs = refs[pos + 1:pos + 1 + n_a] if cast_w else w_refs
    j = pl.program_id(0)
    i = pl.program_id(1)

    if cast_w:
        @pl.when(i == 0)
        def _():
            for w_ref, wbf_ref in zip(w_refs, wbf_refs):
                wbf_ref[...] = w_ref[...].astype(BF16)

    acc = None
    for a_ref, wbf_ref in zip(a_refs, wbf_refs):
        d = jnp.dot(a_ref[...], wbf_ref[...], preferred_element_type=F32)
        acc = d if acc is None else acc + d
    if scale is not None:
        acc = acc * jnp.where(j < n_scaled, scale, 1.0).astype(F32)
    if has_res:
        acc = res_ref[...] + acc
    o_ref[...] = acc.astype(o_ref.dtype)


def matmul(a_list, w, layer, *, n_out, col_off=0, res=None, out_dtype=F32, tm, tn,
           scale=None, n_scaled_cols=0):
    m = a_list[0].shape[0]
    ks = [a.shape[1] for a in a_list]
    assert sum(ks) == w.shape[1] and len(set(ks)) == 1
    assert m % tm == 0 and n_out % tn == 0 and col_off % tn == 0
    coff = col_off // tn
    cast_w = w.dtype != BF16
    in_specs = [pl.BlockSpec((tm, k), lambda j, i: (i, 0)) for k in ks]
    in_specs += [pl.BlockSpec((None, k, tn),
                              functools.partial(lambda j, i, p: (layer, p, j + coff), p=p))
                 for p, k in enumerate(ks)]
    args = list(a_list) + [w] * len(a_list)
    if res is not None:
        in_specs.append(pl.BlockSpec((tm, tn), lambda j, i: (i, j)))
        args.append(res)
    kern = functools.partial(_mm_kernel, n_a=len(a_list), has_res=res is not None, cast_w=cast_w,
                             scale=scale, n_scaled=n_scaled_cols // tn)
    return pl.pallas_call(
        kern,
        grid=(n_out // tn, m // tm),
        in_specs=in_specs,
        out_specs=pl.BlockSpec((tm, tn), lambda j, i: (i, j)),
        out_shape=jax.ShapeDtypeStruct((m, n_out), out_dtype),
        scratch_shapes=[pltpu.VMEM((k, tn), BF16) for k in ks] if cast_w else [],
        compiler_params=_params("arbitrary", "arbitrary"),
        name="matmul",
    )(*args)


def _sb_kernel(q_ref, k_ref, v_ref, u2_ref, nw_ref, o_ref, d_ref, pend_ref, carry_ref, acc_ref,
               *, tq, n_group):
    dh = HEAD_DIM
    qi = pl.program_id(1)
    u2 = u2_ref[...]

    def scores(kb, slot, mask):
        start = pl.multiple_of(kb * tq, tq)
        for g in range(n_group):
            cols = slice(g * dh, (g + 1) * dh)
            q = q_ref[:, cols]
            k = k_ref[pl.ds(start, tq), cols]
            z = lax.dot_general(q, k, (((1,), (1,)), ((), ())), preferred_element_type=F32)
            neg_abs = lax.bitcast_convert_type(
                lax.bitcast_convert_type(z, jnp.uint32) | jnp.uint32(0x80000000), F32)
            sp = jnp.maximum(z, 0.0) + jnp.log(1.0 + jnp.exp2(neg_abs)) * LOG2E
            if mask is not None:
                sp = jnp.where(mask, sp, 0.0)
            hi = sp.astype(BF16)
            lo = (sp - hi.astype(F32)).astype(BF16)
            c = jnp.dot(jnp.concatenate([hi, lo], axis=1), u2, preferred_element_type=F32)
            d = z - c
            if mask is not None:
                d = jnp.where(mask, d, -jnp.inf)
            d_ref[slot, g] = d
            pend_ref[g] = c[:, 0:1]

    def weigh(kb, slot):
        start = pl.multiple_of(kb * tq, tq)
        for g in range(n_group):
            cols = slice(g * dh, (g + 1) * dh)
            v = v_ref[pl.ds(start, tq), cols]
            carry = carry_ref[g]
            w = jnp.exp2(d_ref[slot, g] - carry)
            acc_ref[g] += jnp.dot(w.astype(BF16), v, preferred_element_type=F32)
            carry_ref[g] = carry + pend_ref[g]

    carry_ref[...] = jnp.zeros_like(carry_ref)
    acc_ref[...] = jnp.zeros_like(acc_ref)
    row = lax.broadcasted_iota(jnp.int32, (tq, tq), 0)
    col = lax.broadcasted_iota(jnp.int32, (tq, tq), 1)
    scores(qi, 0, col < row)

    def body(n, _):
        slot = lax.rem(n, 2)
        weigh(qi - n, slot)
        scores(qi - 1 - n, 1 - slot, None)
        return 0

    lax.fori_loop(0, qi, body, 0)
    weigh(0, lax.rem(qi, 2))
    for g in range(n_group):
        acc = acc_ref[g]
        cols = slice(g * dh, (g + 1) * dh)
        o = acc * lax.rsqrt(jnp.mean(acc * acc, axis=-1, keepdims=True) + EPS)
        o_ref[:, cols] = (o * nw_ref[:, cols]).astype(o_ref.dtype)


def sb_attention(qkv, norm_w, n_heads, tq=256, n_group=4):
    t = qkv.shape[0]
    gw = n_group * HEAD_DIM
    ng = n_heads // n_group
    idx = jnp.arange(tq)
    u = (idx[:, None] >= idx[None, :]).astype(BF16)
    u2 = jnp.concatenate([u, u], axis=0)
    return pl.pallas_call(
        functools.partial(_sb_kernel, tq=tq, n_group=n_group),
        grid=(ng, t // tq),
        in_specs=[pl.BlockSpec((tq, gw), lambda h, i: (i, h)),
                  pl.BlockSpec((t, gw), lambda h, i: (0, ng + h)),
                  pl.BlockSpec((t, gw), lambda h, i: (0, 2 * ng + h)),
                  pl.BlockSpec((2 * tq, tq), lambda h, i: (0, 0)),
                  pl.BlockSpec((1, gw), lambda h, i: (0, h))],
        out_specs=pl.BlockSpec((tq, gw), lambda h, i: (i, h)),
        out_shape=jax.ShapeDtypeStruct((t, n_heads * HEAD_DIM), BF16),
        scratch_shapes=[pltpu.VMEM((2, n_group, tq, tq), F32),
                        pltpu.VMEM((n_group, tq, 1), F32),
                        pltpu.VMEM((n_group, tq, 1), F32),
                        pltpu.VMEM((n_group, tq, HEAD_DIM), F32)],
        compiler_params=_params("arbitrary", "arbitrary"),
        name="sb_attention",
    )(qkv, qkv, qkv, u2, norm_w.reshape(1, n_heads * HEAD_DIM))


def _lb_kernel(p_ref, lb_ref):
    p = p_ref[...]
    depth = p.shape[0]
    e = jnp.exp(p - jnp.max(p, axis=0, keepdims=True))
    sm = e / jnp.sum(e, axis=0, keepdims=True)
    c = sm[0:1]
    first = c
    rows = [c - first]
    for l in range(1, depth):
        c = c + sm[l:l + 1]
        rows.append(c - first)
    lb_ref[...] = jnp.concatenate(rows, axis=0)


def hgrn_lower_bounds(p):
    return pl.pallas_call(
        _lb_kernel,
        out_shape=jax.ShapeDtypeStruct(p.shape, F32),
        name="hgrn_lower_bounds",
    )(p)


def _hgrn_kernel(q_ref, f_ref, i_ref, g_ref, lb_ref, nw_ref, o_ref, st_ref, *, n_heads):
    dh = HEAD_DIM
    cc = HG_CHUNK
    nsub = cc // HG_SUB

    @pl.when(pl.program_id(0) == 0)
    def _():
        st_ref[...] = jnp.zeros_like(st_ref)

    fz = f_ref[...]
    lb = lb_ref[...]
    e = jnp.exp(-jnp.abs(fz))
    log_sig = jnp.minimum(fz, 0.0) - jnp.log1p(e)
    r = 1.0 / (1.0 + e)
    sig_neg = jnp.where(fz >= 0.0, e * r, r)
    a = jnp.log(lb)
    b2 = jnp.log1p(-lb) + log_sig
    log_f = jnp.maximum(a, b2) + jnp.log1p(jnp.exp(-jnp.abs(a - b2)))
    kk = (1.0 - lb) * sig_neg
    qz = q_ref[...]
    qq = qz * _sigmoid(qz)
    gz = g_ref[...]
    gate = gz * _sigmoid(gz)

    row_w = lax.broadcasted_iota(jnp.int32, log_f.shape, 0)
    b = log_f
    s = 1
    while s < cc:
        b = b + jnp.where(row_w >= s, pltpu.roll(b, s, axis=0), 0.0)
        s *= 2

    row = lax.broadcasted_iota(jnp.int32, (cc, dh), 0)
    r2 = lax.broadcasted_iota(jnp.int32, (cc, cc), 0)
    c2 = lax.broadcasted_iota(jnp.int32, (cc, cc), 1)
    causal = c2 <= r2
    zeros_sub = jnp.zeros((HG_SUB, dh), BF16)

    for h in range(n_heads):
        sl = slice(h * dh, (h + 1) * dh)
        bh, qh, kh, vh = b[:, sl], qq[:, sl], kk[:, sl], i_ref[:, sl]
        vb = vh.astype(BF16)
        q_parts, k_rows = [], []
        for jb in range(nsub):
            anchor = bh[jb * HG_SUB:jb * HG_SUB + 1]
            qd = qh * jnp.exp(jnp.minimum(bh - anchor, 0.0))
            q_parts.append(jnp.where(row >= jb * HG_SUB, qd, 0.0).astype(BF16))
            rows = slice(jb * HG_SUB, (jb + 1) * HG_SUB)
            ks = kh[rows] * jnp.exp(jnp.minimum(anchor - bh[rows], HG_EXP_CLAMP))
            k_rows.append(jnp.concatenate(
                [zeros_sub] * jb + [ks.astype(BF16)] + [zeros_sub] * (nsub - 1 - jb), axis=1))
        q_cat = jnp.concatenate(q_parts, axis=1)
        k_cat = jnp.concatenate(k_rows, axis=0)
        scores = lax.dot_general(q_cat, k_cat, (((1,), (1,)), ((), ())),
                                 preferred_element_type=F32)
        p = jnp.where(causal, scores, 0.0).astype(BF16)
        o = jnp.dot(p, vb, preferred_element_type=F32)
        st = st_ref[h]
        q0 = (qh * jnp.exp(bh)).astype(BF16)
        o = o + lax.dot_general(q0, st.astype(BF16), (((1,), (1,)), ((), ())),
                                preferred_element_type=F32)
        b_last = bh[cc - 1:cc]
        k_end = (kh * jnp.exp(b_last - bh)).astype(BF16)
        st_ref[h] = st * jnp.exp(b_last) + lax.dot_general(
            vb, k_end, (((0,), (0,)), ((), ())), preferred_element_type=F32)
        o = o * lax.rsqrt(jnp.mean(o * o, axis=-1, keepdims=True) + EPS)
        o_ref[:, sl] = (o * nw_ref[:, sl] * gate[:, sl]).astype(o_ref.dtype)


def hgrn2(hg, lb, norm_w, n_heads):
    t = hg.shape[0]
    w = n_heads * HEAD_DIM
    cc = HG_CHUNK
    col = lambda c: pl.BlockSpec((cc, w), lambda i: (i, c))
    vec = pl.BlockSpec((1, w), lambda i: (0, 0))
    return pl.pallas_call(
        functools.partial(_hgrn_kernel, n_heads=n_heads),
        grid=(t // cc,),
        in_specs=[col(0), col(1), col(2), col(3), vec, vec],
        out_specs=pl.BlockSpec((cc, w), lambda i: (i, 0)),
        out_shape=jax.ShapeDtypeStruct((t, w), BF16),
        scratch_shapes=[pltpu.VMEM((n_heads, HEAD_DIM, HEAD_DIM), F32)],
        compiler_params=_params("arbitrary"),
        name="hgrn2",
    )(hg, hg, hg, hg, lb.reshape(1, w), norm_w.reshape(1, w))


def _up_kernel(a_ref, wg_ref, wu_ref, cwg_ref, cwu_ref, cbg_ref, cbu_ref, o_ref,
               wg_bf, wu_bf, tail_ref, *, tm, sub):
    i = pl.program_id(1)
    tn = o_ref.shape[1]
    sl = SUBLANES

    @pl.when(i == 0)
    def _():
        wg_bf[...] = wg_ref[...].astype(BF16)
        wu_bf[...] = wu_ref[...].astype(BF16)
        tail_ref[...] = jnp.zeros_like(tail_ref)

    srow = lax.broadcasted_iota(jnp.int32, (sub // sl, sl, tn), 1)

    def conv(x, tail, cw_ref, cb_ref):
        x3 = x.reshape(sub // sl, sl, tn)
        both = jnp.concatenate([tail[None], x3], axis=0)
        r1 = pltpu.roll(both, 1, axis=1)
        r2 = pltpu.roll(both, 2, axis=1)
        x1 = jnp.where(srow < 1, r1[:-1], r1[1:]).reshape(sub, tn)
        x2 = jnp.where(srow < 2, r2[:-1], r2[1:]).reshape(sub, tn)
        y = cb_ref[...] + cw_ref[0:1] * x2
        y = y + cw_ref[1:2] * x1
        y = y + cw_ref[2:3] * x
        return y, x3[-1]

    tg, tu = tail_ref[0], tail_ref[1]
    for r in range(tm // sub):
        a = a_ref[r * sub:(r + 1) * sub, :]
        g = jnp.dot(a, wg_bf[...], preferred_element_type=F32)
        u = jnp.dot(a, wu_bf[...], preferred_element_type=F32)
        gc, tg = conv(g, tg, cwg_ref, cbg_ref)
        uc, tu = conv(u, tu, cwu_ref, cbu_ref)
        o_ref[r * sub:(r + 1) * sub, :] = (gc * _sigmoid(gc) * uc).astype(o_ref.dtype)
    tail_ref[0] = tg
    tail_ref[1] = tu


def conv_gated_up(a, w_up, layer, conv_w, conv_b, *, tm, tn, sub=256):
    m, k = a.shape
    d_ff = w_up.shape[2] // 2
    assert d_ff % tn == 0 and m % tm == 0 and tm % sub == 0 and sub % SUBLANES == 0
    assert CONV_WIDTH - 1 <= SUBLANES
    nj = d_ff // tn
    cb = conv_b.reshape(1, 2 * d_ff)
    return pl.pallas_call(
        functools.partial(_up_kernel, tm=tm, sub=sub),
        grid=(nj, m // tm),
        in_specs=[pl.BlockSpec((tm, k), lambda j, i: (i, 0)),
                  pl.BlockSpec((None, k, tn), lambda j, i: (layer, 0, j)),
                  pl.BlockSpec((None, k, tn), lambda j, i: (layer, 0, j + nj)),
                  pl.BlockSpec((CONV_WIDTH, tn), lambda j, i: (0, j)),
                  pl.BlockSpec((CONV_WIDTH, tn), lambda j, i: (0, j + nj)),
                  pl.BlockSpec((1, tn), lambda j, i: (0, j)),
                  pl.BlockSpec((1, tn), lambda j, i: (0, j + nj))],
        out_specs=pl.BlockSpec((tm, tn), lambda j, i: (i, j)),
        out_shape=jax.ShapeDtypeStruct((m, d_ff), BF16),
        scratch_shapes=[pltpu.VMEM((k, tn), BF16), pltpu.VMEM((k, tn), BF16),
                        pltpu.VMEM((2, SUBLANES, tn), F32)],
        compiler_params=_params("arbitrary", "arbitrary"),
        name="conv_gated_up",
    )(a, w_up, w_up, conv_w, conv_w, cb, cb)


def kernel(x, norm1_w, w_in, sb_norm_w, hg_lb_param, hg_norm_w, w_out, norm2_w, w_up, conv_w,
           conv_b, w_down, final_norm_w):
    batch, seq, d_model = x.shape
    assert batch == 1
    depth = w_in.shape[0]
    sb_width = sb_norm_w.shape[1]
    hg_width = hg_norm_w.shape[1]
    sb_heads = sb_width // HEAD_DIM
    hg_heads = hg_width // HEAD_DIM
    scale = HEAD_DIM ** -0.5 * LOG2E

    lb_all = hgrn_lower_bounds(hg_lb_param)
    w_down_bf = w_down.astype(BF16)
    h = x.reshape(seq, d_model)
    for l in range(depth):
        hn = rmsnorm(h, norm1_w[l], BF16)
        qkv = matmul([hn], w_in, l, n_out=3 * sb_width, out_dtype=BF16, tm=1024, tn=512,
                     scale=scale, n_scaled_cols=sb_width)
        hg = matmul([hn], w_in, l, n_out=4 * hg_width, col_off=3 * sb_width, tm=1024, tn=512)
        o_sb = sb_attention(qkv, sb_norm_w[l], sb_heads)
        o_hg = hgrn2(hg, lb_all[l], hg_norm_w[l], hg_heads)
        h = matmul([o_sb, o_hg], w_out, l, n_out=d_model, res=h, tm=1024, tn=512)
        hn = rmsnorm(h, norm2_w[l], BF16)
        act = conv_gated_up(hn, w_up, l, conv_w[l], conv_b[l], tm=1024, tn=256)
        h = matmul([act], w_down_bf, l, n_out=d_model, res=h, tm=512, tn=512)
    return rmsnorm(h, final_norm_w, F32).reshape(batch, seq, d_model)
```

```python
import functools

import jax
import jax.numpy as jnp
from jax import lax
from jax.experimental import pallas as pl
from jax.experimental.pallas import tpu as pltpu

F32 = jnp.float32
BF16 = jnp.bfloat16

HEAD_DIM = 128
SUBLANES = 8
CONV_WIDTH = 3
EPS = 1e-6
LOG2E = 1.4426950408889634

V7X_VMEM_LIMIT_BYTES = 56 * 1024 * 1024
HG_CHUNK = 64
HG_SUB = 16
HG_EXP_CLAMP = 80.0


def _params(*sem):
    return pltpu.CompilerParams(dimension_semantics=sem, vmem_limit_bytes=V7X_VMEM_LIMIT_BYTES)


def _sigmoid(x):
    return 1.0 / (1.0 + jnp.exp(-x))


def _rmsnorm_kernel(x_ref, w_ref, o_ref):
    x = x_ref[...]
    y = x * lax.rsqrt(jnp.mean(x * x, axis=-1, keepdims=True) + EPS)
    o_ref[...] = (y * w_ref[...]).astype(o_ref.dtype)


def rmsnorm(x, w, out_dtype, tm=256):
    t, d = x.shape
    return pl.pallas_call(
        _rmsnorm_kernel,
        grid=(t // tm,),
        in_specs=[pl.BlockSpec((tm, d), lambda i: (i, 0)),
                  pl.BlockSpec((1, d), lambda i: (0, 0))],
        out_specs=pl.BlockSpec((tm, d), lambda i: (i, 0)),
        out_shape=jax.ShapeDtypeStruct((t, d), out_dtype),
        compiler_params=_params("arbitrary"),
        name="rmsnorm",
    )(x, w.reshape(1, d))


def _mm_kernel(*refs, n_a, has_res, cast_w, scale, n_scaled):
    a_refs = refs[:n_a]
    w_refs = refs[n_a:2 * n_a]
    pos = 2 * n_a
    res_ref = refs[pos] if has_res else None
    pos += int(has_res)
    o_ref = refs[pos]
    wbf_refs = refs[pos + 1:pos + 1 + n_a] if cast_w else w_refs
    j = pl.program_id(0)
    i = pl.program_id(1)

    if cast_w:
        @pl.when(i == 0)
        def _():
            for w_ref, wbf_ref in zip(w_refs, wbf_refs):
                wbf_ref[...] = w_ref[...].astype(BF16)

    acc = None
    for a_ref, wbf_ref in zip(a_refs, wbf_refs):
        d = jnp.dot(a_ref[...], wbf_ref[...], preferred_element_type=F32)
        acc = d if acc is None else acc + d
    if scale is not None:
        acc = acc * jnp.where(j < n_scaled, scale, 1.0).astype(F32)
    if has_res:
        acc = res_ref[...] + acc
    o_ref[...] = acc.astype(o_ref.dtype)


def matmul(a_list, w, layer, *, n_out, col_off=0, res=None, out_dtype=F32, tm, tn,
           scale=None, n_scaled_cols=0):
    m = a_list[0].shape[0]
    ks = [a.shape[1] for a in a_list]
    assert sum(ks) == w.shape[1] and len(set(ks)) == 1
    assert m % tm == 0 and n_out % tn == 0 and col_off % tn == 0
    coff = col_off // tn
    cast_w = w.dtype != BF16
    in_specs = [pl.BlockSpec((tm, k), lambda j, i: (i, 0)) for k in ks]
    in_specs += [pl.BlockSpec((None, k, tn),
                              functools.partial(lambda j, i, p: (layer, p, j + coff), p=p))
                 for p, k in enumerate(ks)]
    args = list(a_list) + [w] * len(a_list)
    if res is not None:
        in_specs.append(pl.BlockSpec((tm, tn), lambda j, i: (i, j)))
        args.append(res)
    kern = functools.partial(_mm_kernel, n_a=len(a_list), has_res=res is not None, cast_w=cast_w,
                             scale=scale, n_scaled=n_scaled_cols // tn)
    return pl.pallas_call(
        kern,
        grid=(n_out // tn, m // tm),
        in_specs=in_specs,
        out_specs=pl.BlockSpec((tm, tn), lambda j, i: (i, j)),
        out_shape=jax.ShapeDtypeStruct((m, n_out), out_dtype),
        scratch_shapes=[pltpu.VMEM((k, tn), BF16) for k in ks] if cast_w else [],
        compiler_params=_params("arbitrary", "arbitrary"),
        name="matmul",
    )(*args)


def _sb_kernel(q_ref, k_ref, v_ref, u2_ref, nw_ref, o_ref, z_ref, d_ref, pend_ref,
               carry_ref, acc_ref, *, tq, n_group):
    dh = HEAD_DIM
    qi = pl.program_id(1)
    nb = qi + 1
    u2 = u2_ref[...]

    def key_rows(j):
        return pl.ds(pl.multiple_of((qi - j) * tq, tq), tq)

    def logits(j, slot):
        rows = key_rows(j)
        for g in range(n_group):
            cols = slice(g * dh, (g + 1) * dh)
            q = q_ref[:, cols]
            z_ref[slot, g] = lax.dot_general(q, k_ref[rows, cols], (((1,), (1,)), ((), ())),
                                             preferred_element_type=F32)

    def sums(slot, mask):
        for g in range(n_group):
            z = z_ref[slot, g]
            neg_abs = lax.bitcast_convert_type(
                lax.bitcast_convert_type(z, jnp.uint32) | jnp.uint32(0x80000000), F32)
            sp = jnp.maximum(z, 0.0) + jnp.log(1.0 + jnp.exp2(neg_abs)) * LOG2E
            if mask is not None:
                sp = jnp.where(mask, sp, 0.0)
            hi = sp.astype(BF16)
            lo = (sp - hi.astype(F32)).astype(BF16)
            c = jnp.dot(jnp.concatenate([hi, lo], axis=1), u2, preferred_element_type=F32)
            d = z - c
            if mask is not None:
                d = jnp.where(mask, d, -jnp.inf)
            d_ref[slot, g] = d
            pend_ref[g] = jnp.broadcast_to(c[:, 0:1], (tq, dh))

    def weigh(j, slot):
        rows = key_rows(j)
        for g in range(n_group):
            cols = slice(g * dh, (g + 1) * dh)
            carry = carry_ref[g]
            w = jnp.exp2(d_ref[slot, g] - jnp.concatenate([carry] * (tq // dh), axis=1))
            acc_ref[g] += jnp.dot(w.astype(BF16), v_ref[rows, cols], preferred_element_type=F32)
            carry_ref[g] = carry + pend_ref[g]

    carry_ref[...] = jnp.zeros_like(carry_ref)
    acc_ref[...] = jnp.zeros_like(acc_ref)
    row = lax.broadcasted_iota(jnp.int32, (tq, tq), 0)
    col = lax.broadcasted_iota(jnp.int32, (tq, tq), 1)
    strict = col < row

    @pl.when(qi == 0)
    def _():
        logits(0, 0)
        sums(0, strict)
        weigh(0, 0)

    @pl.when(qi > 0)
    def _():
        logits(0, 0)
        sums(0, strict)
        logits(1, 1)

    def step(i, even):
        a, b = (0, 1) if even else (1, 0)
        weigh(i - 2, a)
        sums(b, None)
        logits(i, a)

    n_steady = nb - 2

    def pair(p, _):
        i = 2 + 2 * p
        step(i, True)
        step(i + 1, False)
        return 0

    lax.fori_loop(0, n_steady // 2, pair, 0)

    @pl.when(lax.rem(n_steady, 2) == 1)
    def _():
        step(nb - 1, True)

    @pl.when(qi > 0)
    def _():
        weigh(nb - 2, lax.rem(nb, 2))
        sums(lax.rem(nb - 1, 2), None)
        weigh(nb - 1, lax.rem(nb - 1, 2))

    for g in range(n_group):
        acc = acc_ref[g]
        cols = slice(g * dh, (g + 1) * dh)
        o = acc * lax.rsqrt(jnp.mean(acc * acc, axis=-1, keepdims=True) + EPS)
        o_ref[:, cols] = (o * nw_ref[:, cols]).astype(o_ref.dtype)


def sb_attention(qkv, norm_w, n_heads, tq=256, n_group=4):
    t = qkv.shape[0]
    assert tq % HEAD_DIM == 0 and t % tq == 0 and n_heads % n_group == 0
    gw = n_group * HEAD_DIM
    ng = n_heads // n_group
    idx = jnp.arange(tq)
    u = (idx[:, None] >= idx[None, :]).astype(BF16)
    u2 = jnp.concatenate([u, u], axis=0)
    return pl.pallas_call(
        functools.partial(_sb_kernel, tq=tq, n_group=n_group),
        grid=(ng, t // tq),
        in_specs=[pl.BlockSpec((tq, gw), lambda h, i: (i, h)),
                  pl.BlockSpec((t, gw), lambda h, i: (0, ng + h)),
                  pl.BlockSpec((t, gw), lambda h, i: (0, 2 * ng + h)),
                  pl.BlockSpec((2 * tq, tq), lambda h, i: (0, 0)),
                  pl.BlockSpec((1, gw), lambda h, i: (0, h))],
        out_specs=pl.BlockSpec((tq, gw), lambda h, i: (i, h)),
        out_shape=jax.ShapeDtypeStruct((t, n_heads * HEAD_DIM), BF16),
        scratch_shapes=[pltpu.VMEM((2, n_group, tq, tq), F32),
                        pltpu.VMEM((2, n_group, tq, tq), F32),
                        pltpu.VMEM((n_group, tq, HEAD_DIM), F32),
                        pltpu.VMEM((n_group, tq, HEAD_DIM), F32),
                        pltpu.VMEM((n_group, tq, HEAD_DIM), F32)],
        compiler_params=_params("arbitrary", "arbitrary"),
        name="sb_attention",
    )(qkv, qkv, qkv, u2, norm_w.reshape(1, n_heads * HEAD_DIM))


def _lb_kernel(p_ref, lb_ref):
    p = p_ref[...]
    depth = p.shape[0]
    e = jnp.exp(p - jnp.max(p, axis=0, keepdims=True))
    sm = e / jnp.sum(e, axis=0, keepdims=True)
    c = sm[0:1]
    first = c
    rows = [c - first]
    for l in range(1, depth):
        c = c + sm[l:l + 1]
        rows.append(c - first)
    lb_ref[...] = jnp.concatenate(rows, axis=0)


def hgrn_lower_bounds(p):
    return pl.pallas_call(
        _lb_kernel,
        out_shape=jax.ShapeDtypeStruct(p.shape, F32),
        name="hgrn_lower_bounds",
    )(p)


def _hgrn_kernel(q_ref, f_ref, i_ref, g_ref, lb_ref, nw_ref, o_ref, st_ref, *, n_heads):
    dh = HEAD_DIM
    cc = HG_CHUNK
    nsub = cc // HG_SUB

    @pl.when(pl.program_id(0) == 0)
    def _():
        st_ref[...] = jnp.zeros_like(st_ref)

    fz = f_ref[...]
    lb = lb_ref[...]
    e = jnp.exp(-jnp.abs(fz))
    log_sig = jnp.minimum(fz, 0.0) - jnp.log1p(e)
    r = 1.0 / (1.0 + e)
    sig_neg = jnp.where(fz >= 0.0, e * r, r)
    a = jnp.log(lb)
    b2 = jnp.log1p(-lb) + log_sig
    log_f = jnp.maximum(a, b2) + jnp.log1p(jnp.exp(-jnp.abs(a - b2)))
    kk = (1.0 - lb) * sig_neg
    qz = q_ref[...]
    qq = qz * _sigmoid(qz)
    gz = g_ref[...]
    gate = gz * _sigmoid(gz)

    row_w = lax.broadcasted_iota(jnp.int32, log_f.shape, 0)
    b = log_f
    s = 1
    while s < cc:
        b = b + jnp.where(row_w >= s, pltpu.roll(b, s, axis=0), 0.0)
        s *= 2

    r2 = lax.broadcasted_iota(jnp.int32, (cc, cc), 0)
    c2 = lax.broadcasted_iota(jnp.int32, (cc, cc), 1)
    causal = c2 <= r2
    zeros_sub = jnp.zeros((HG_SUB, dh), BF16)

    for h in range(n_heads):
        sl = slice(h * dh, (h + 1) * dh)
        bh, qh, kh, vh = b[:, sl], qq[:, sl], kk[:, sl], i_ref[:, sl]
        vb = vh.astype(BF16)
        q_parts, k_rows = [], []
        for jb in range(nsub):
            lo = jb * HG_SUB
            anchor = bh[lo:lo + 1]
            qd = (qh[lo:] * jnp.exp(bh[lo:] - anchor)).astype(BF16)
            q_parts.append(jnp.concatenate([jnp.zeros((lo, dh), BF16), qd], axis=0) if lo else qd)
            rows = slice(lo, lo + HG_SUB)
            ks = kh[rows] * jnp.exp(jnp.minimum(anchor - bh[rows], HG_EXP_CLAMP))
            k_rows.append(jnp.concatenate(
                [zeros_sub] * jb + [ks.astype(BF16)] + [zeros_sub] * (nsub - 1 - jb), axis=1))
        q_cat = jnp.concatenate(q_parts, axis=1)
        k_cat = jnp.concatenate(k_rows, axis=0)
        scores = lax.dot_general(q_cat, k_cat, (((1,), (1,)), ((), ())),
                                 preferred_element_type=F32)
        p = jnp.where(causal, scores, 0.0).astype(BF16)
        o = jnp.dot(p, vb, preferred_element_type=F32)
        st = st_ref[h]
        q0 = (qh * jnp.exp(bh)).astype(BF16)
        o = o + lax.dot_general(q0, st.astype(BF16), (((1,), (1,)), ((), ())),
                                preferred_element_type=F32)
        b_last = bh[cc - 1:cc]
        k_end = (kh * jnp.exp(b_last - bh)).astype(BF16)
        st_ref[h] = st * jnp.exp(b_last) + lax.dot_general(
            vb, k_end, (((0,), (0,)), ((), ())), preferred_element_type=F32)
        o = o * lax.rsqrt(jnp.mean(o * o, axis=-1, keepdims=True) + EPS)
        o_ref[:, sl] = (o * nw_ref[:, sl] * gate[:, sl]).astype(o_ref.dtype)


def hgrn2(hg, lb, norm_w, n_heads):
    t = hg.shape[0]
    w = n_heads * HEAD_DIM
    cc = HG_CHUNK
    col = lambda c: pl.BlockSpec((cc, w), lambda i: (i, c))
    vec = pl.BlockSpec((1, w), lambda i: (0, 0))
    return pl.pallas_call(
        functools.partial(_hgrn_kernel, n_heads=n_heads),
        grid=(t // cc,),
        in_specs=[col(0), col(1), col(2), col(3), vec, vec],
        out_specs=pl.BlockSpec((cc, w), lambda i: (i, 0)),
        out_shape=jax.ShapeDtypeStruct((t, w), BF16),
        scratch_shapes=[pltpu.VMEM((n_heads, HEAD_DIM, HEAD_DIM), F32)],
        compiler_params=_params("arbitrary"),
        name="hgrn2",
    )(hg, hg, hg, hg, lb.reshape(1, w), norm_w.reshape(1, w))


def _up_kernel(a_ref, wg_ref, wu_ref, cwg_ref, cwu_ref, cbg_ref, cbu_ref, o_ref,
               wg_bf, wu_bf, tail_ref, *, tm, sub):
    i = pl.program_id(1)
    tn = o_ref.shape[1]
    sl = SUBLANES

    @pl.when(i == 0)
    def _():
        wg_bf[...] = wg_ref[...].astype(BF16)
        wu_bf[...] = wu_ref[...].astype(BF16)
        tail_ref[...] = jnp.zeros_like(tail_ref)

    srow = lax.broadcasted_iota(jnp.int32, (sub // sl, sl, tn), 1)

    def conv(x, tail, cw_ref, cb_ref):
        x3 = x.reshape(sub // sl, sl, tn)
        both = jnp.concatenate([tail[None], x3], axis=0)
        r1 = pltpu.roll(both, 1, axis=1)
        r2 = pltpu.roll(both, 2, axis=1)
        x1 = jnp.where(srow < 1, r1[:-1], r1[1:]).reshape(sub, tn)
        x2 = jnp.where(srow < 2, r2[:-1], r2[1:]).reshape(sub, tn)
        y = cb_ref[...] + cw_ref[0:1] * x2
        y = y + cw_ref[1:2] * x1
        y = y + cw_ref[2:3] * x
        return y, x3[-1]

    tg, tu = tail_ref[0], tail_ref[1]
    for r in range(tm // sub):
        a = a_ref[r * sub:(r + 1) * sub, :]
        g = jnp.dot(a, wg_bf[...], preferred_element_type=F32)
        u = jnp.dot(a, wu_bf[...], preferred_element_type=F32)
        gc, tg = conv(g, tg, cwg_ref, cbg_ref)
        uc, tu = conv(u, tu, cwu_ref, cbu_ref)
        o_ref[r * sub:(r + 1) * sub, :] = (gc * _sigmoid(gc) * uc).astype(o_ref.dtype)
    tail_ref[0] = tg
    tail_ref[1] = tu


def conv_gated_up(a, w_up, layer, conv_w, conv_b, *, tm, tn, sub=256):
    m, k = a.shape
    d_ff = w_up.shape[2] // 2
    assert d_ff % tn == 0 and m % tm == 0 and tm % sub == 0 and sub % SUBLANES == 0
    assert CONV_WIDTH - 1 <= SUBLANES
    nj = d_ff // tn
    cb = conv_b.reshape(1, 2 * d_ff)
    return pl.pallas_call(
        functools.partial(_up_kernel, tm=tm, sub=sub),
        grid=(nj, m // tm),
        in_specs=[pl.BlockSpec((tm, k), lambda j, i: (i, 0)),
                  pl.BlockSpec((None, k, tn), lambda j, i: (layer, 0, j)),
                  pl.BlockSpec((None, k, tn), lambda j, i: (layer, 0, j + nj)),
                  pl.BlockSpec((CONV_WIDTH, tn), lambda j, i: (0, j)),
                  pl.BlockSpec((CONV_WIDTH, tn), lambda j, i: (0, j + nj)),
                  pl.BlockSpec((1, tn), lambda j, i: (0, j)),
                  pl.BlockSpec((1, tn), lambda j, i: (0, j + nj))],
        out_specs=pl.BlockSpec((tm, tn), lambda j, i: (i, j)),
        out_shape=jax.ShapeDtypeStruct((m, d_ff), BF16),
        scratch_shapes=[pltpu.VMEM((k, tn), BF16), pltpu.VMEM((k, tn), BF16),
                        pltpu.VMEM((2, SUBLANES, tn), F32)],
        compiler_params=_params("arbitrary", "arbitrary"),
        name="conv_gated_up",
    )(a, w_up, w_up, conv_w, conv_w, cb, cb)


def kernel(x, norm1_w, w_in, sb_norm_w, hg_lb_param, hg_norm_w, w_out, norm2_w, w_up, conv_w,
           conv_b, w_down, final_norm_w):
    batch, seq, d_model = x.shape
    assert batch == 1
    depth = w_in.shape[0]
    sb_width = sb_norm_w.shape[1]
    hg_width = hg_norm_w.shape[1]
    sb_heads = sb_width // HEAD_DIM
    hg_heads = hg_width // HEAD_DIM
    scale = HEAD_DIM ** -0.5 * LOG2E

    lb_all = hgrn_lower_bounds(hg_lb_param)
    w_down_bf = w_down.astype(BF16)
    h = x.reshape(seq, d_model)
    for l in range(depth):
        hn = rmsnorm(h, norm1_w[l], BF16)
        qkv = matmul([hn], w_in, l, n_out=3 * sb_width, out_dtype=BF16, tm=1024, tn=512,
                     scale=scale, n_scaled_cols=sb_width)
        hg = matmul([hn], w_in, l, n_out=4 * hg_width, col_off=3 * sb_width, tm=1024, tn=512)
        o_sb = sb_attention(qkv, sb_norm_w[l], sb_heads)
        o_hg = hgrn2(hg, lb_all[l], hg_norm_w[l], hg_heads)
        h = matmul([o_sb, o_hg], w_out, l, n_out=d_model, res=h, tm=1024, tn=512)
        hn = rmsnorm(h, norm2_w[l], BF16)
        act = conv_gated_up(hn, w_up, l, conv_w[l], conv_b[l], tm=1024, tn=256)
        h = matmul([act], w_down_bf, l, n_out=d_model, res=h, tm=512, tn=512)
    return rmsnorm(h, final_norm_w, F32).reshape(batch, seq, d_model)
```

```python
import functools

import jax
import jax.numpy as jnp
from jax import lax
from jax.experimental import pallas as pl
from jax.experimental.pallas import tpu as pltpu

F32 = jnp.float32
BF16 = jnp.bfloat16

HEAD_DIM = 128
SUBLANES = 8
CONV_WIDTH = 3
EPS = 1e-6
LOG2E = 1.4426950408889634

V7X_VMEM_LIMIT_BYTES = 56 * 1024 * 1024
HG_CHUNK = 64
HG_SUB = 16
HG_CHUNKS_PER_STEP = 4
HG_EXP_CLAMP = 80.0


def _params(*sem):
    return pltpu.CompilerParams(dimension_semantics=sem, vmem_limit_bytes=V7X_VMEM_LIMIT_BYTES)


def _sigmoid(x):
    return 1.0 / (1.0 + jnp.exp(-x))


def _rmsnorm_kernel(x_ref, w_ref, o_ref):
    x = x_ref[...]
    y = x * lax.rsqrt(jnp.mean(x * x, axis=-1, keepdims=True) + EPS)
    o_ref[...] = (y * w_ref[...]).astype(o_ref.dtype)


def rmsnorm(x, w, out_dtype, tm=256):
    t, d = x.shape
    return pl.pallas_call(
        _rmsnorm_kernel,
        grid=(t // tm,),
        in_specs=[pl.BlockSpec((tm, d), lambda i: (i, 0)),
                  pl.BlockSpec((1, d), lambda i: (0, 0))],
        out_specs=pl.BlockSpec((tm, d), lambda i: (i, 0)),
        out_shape=jax.ShapeDtypeStruct((t, d), out_dtype),
        compiler_params=_params("arbitrary"),
        name="rmsnorm",
    )(x, w.reshape(1, d))


def _mm_kernel(*refs, n_a, has_res, cast_w, scale, n_scaled):
    a_refs = refs[:n_a]
    w_refs = refs[n_a:2 * n_a]
    pos = 2 * n_a
    res_ref = refs[pos] if has_res else None
    pos += int(has_res)
    o_ref = refs[pos]
    wbf_refs = refs[pos + 1:pos + 1 + n_a] if cast_w else w_refs
    j = pl.program_id(0)
    i = pl.program_id(1)

    if cast_w:
        @pl.when(i == 0)
        def _():
            for w_ref, wbf_ref in zip(w_refs, wbf_refs):
                wbf_ref[...] = w_ref[...].astype(BF16)

    acc = None
    for a_ref, wbf_ref in zip(a_refs, wbf_refs):
        d = jnp.dot(a_ref[...], wbf_ref[...], preferred_element_type=F32)
        acc = d if acc is None else acc + d
    if scale is not None:
        acc = acc * jnp.where(j < n_scaled, scale, 1.0).astype(F32)
    if has_res:
        acc = res_ref[...] + acc
    o_ref[...] = acc.astype(o_ref.dtype)


def matmul(a_list, w, layer, *, n_out, col_off=0, res=None, out_dtype=F32, tm, tn,
           scale=None, n_scaled_cols=0):
    m = a_list[0].shape[0]
    ks = [a.shape[1] for a in a_list]
    assert sum(ks) == w.shape[1] and len(set(ks)) == 1
    assert m % tm == 0 and n_out % tn == 0 and col_off % tn == 0
    coff = col_off // tn
    cast_w = w.dtype != BF16
    in_specs = [pl.BlockSpec((tm, k), lambda j, i: (i, 0)) for k in ks]
    in_specs += [pl.BlockSpec((None, k, tn),
                              functools.partial(lambda j, i, p: (layer, p, j + coff), p=p))
                 for p, k in enumerate(ks)]
    args = list(a_list) + [w] * len(a_list)
    if res is not None:
        in_specs.append(pl.BlockSpec((tm, tn), lambda j, i: (i, j)))
        args.append(res)
    kern = functools.partial(_mm_kernel, n_a=len(a_list), has_res=res is not None, cast_w=cast_w,
                             scale=scale, n_scaled=n_scaled_cols // tn)
    return pl.pallas_call(
        kern,
        grid=(n_out // tn, m // tm),
        in_specs=in_specs,
        out_specs=pl.BlockSpec((tm, tn), lambda j, i: (i, j)),
        out_shape=jax.ShapeDtypeStruct((m, n_out), out_dtype),
        scratch_shapes=[pltpu.VMEM((k, tn), BF16) for k in ks] if cast_w else [],
        compiler_params=_params("arbitrary", "arbitrary"),
        name="matmul",
    )(*args)


def _sb_kernel(q_ref, k_ref, v_ref, u2_ref, nw_ref, o_ref, z_ref, d_ref, pend_ref,
               carry_ref, acc_ref, *, tq, n_group):
    dh = HEAD_DIM
    qi = pl.program_id(1)
    nb = qi + 1
    u2 = u2_ref[...]

    def key_rows(j):
        return pl.ds(pl.multiple_of((qi - j) * tq, tq), tq)

    def logits(j, slot):
        rows = key_rows(j)
        for g in range(n_group):
            cols = slice(g * dh, (g + 1) * dh)
            q = q_ref[:, cols]
            z_ref[slot, g] = lax.dot_general(q, k_ref[rows, cols], (((1,), (1,)), ((), ())),
                                             preferred_element_type=F32)

    def sums(slot, mask):
        for g in range(n_group):
            z = z_ref[slot, g]
            neg_abs = lax.bitcast_convert_type(
                lax.bitcast_convert_type(z, jnp.uint32) | jnp.uint32(0x80000000), F32)
            sp = jnp.maximum(z, 0.0) + jnp.log(1.0 + jnp.exp2(neg_abs)) * LOG2E
            if mask is not None:
                sp = jnp.where(mask, sp, 0.0)
            hi = sp.astype(BF16)
            lo = (sp - hi.astype(F32)).astype(BF16)
            c = jnp.dot(jnp.concatenate([hi, lo], axis=1), u2, preferred_element_type=F32)
            d = z - c
            if mask is not None:
                d = jnp.where(mask, d, -jnp.inf)
            d_ref[slot, g] = d
            pend_ref[g] = jnp.broadcast_to(c[:, 0:1], (tq, dh))

    def weigh(j, slot):
        rows = key_rows(j)
        for g in range(n_group):
            cols = slice(g * dh, (g + 1) * dh)
            carry = carry_ref[g]
            w = jnp.exp2(d_ref[slot, g] - jnp.concatenate([carry] * (tq // dh), axis=1))
            acc_ref[g] += jnp.dot(w.astype(BF16), v_ref[rows, cols], preferred_element_type=F32)
            carry_ref[g] = carry + pend_ref[g]

    carry_ref[...] = jnp.zeros_like(carry_ref)
    acc_ref[...] = jnp.zeros_like(acc_ref)
    row = lax.broadcasted_iota(jnp.int32, (tq, tq), 0)
    col = lax.broadcasted_iota(jnp.int32, (tq, tq), 1)
    strict = col < row

    @pl.when(qi == 0)
    def _():
        logits(0, 0)
        sums(0, strict)
        weigh(0, 0)

    @pl.when(qi > 0)
    def _():
        logits(0, 0)
        sums(0, strict)
        logits(1, 1)

    def step(i, even):
        a, b = (0, 1) if even else (1, 0)
        weigh(i - 2, a)
        sums(b, None)
        logits(i, a)

    n_steady = nb - 2

    n_quads = jnp.maximum(n_steady, 0) // 4
    n_left = jnp.maximum(n_steady, 0) - 4 * n_quads

    def quad(p, _):
        i = 2 + 4 * p
        step(i, True)
        step(i + 1, False)
        step(i + 2, True)
        step(i + 3, False)
        return 0

    lax.fori_loop(0, n_quads, quad, 0)

    @pl.when(n_left >= 2)
    def _():
        i = 2 + 4 * n_quads
        step(i, True)
        step(i + 1, False)

    @pl.when(lax.rem(n_left, 2) == 1)
    def _():
        step(nb - 1, True)

    @pl.when(qi > 0)
    def _():
        weigh(nb - 2, lax.rem(nb, 2))
        sums(lax.rem(nb - 1, 2), None)
        weigh(nb - 1, lax.rem(nb - 1, 2))

    for g in range(n_group):
        acc = acc_ref[g]
        cols = slice(g * dh, (g + 1) * dh)
        o = acc * lax.rsqrt(jnp.mean(acc * acc, axis=-1, keepdims=True) + EPS)
        o_ref[:, cols] = (o * nw_ref[:, cols]).astype(o_ref.dtype)


def sb_attention(qkv, norm_w, n_heads, tq=256, n_group=4):
    t = qkv.shape[0]
    assert tq % HEAD_DIM == 0 and t % tq == 0 and n_heads % n_group == 0
    gw = n_group * HEAD_DIM
    ng = n_heads // n_group
    idx = jnp.arange(tq)
    u = (idx[:, None] >= idx[None, :]).astype(BF16)
    u2 = jnp.concatenate([u, u], axis=0)
    return pl.pallas_call(
        functools.partial(_sb_kernel, tq=tq, n_group=n_group),
        grid=(ng, t // tq),
        in_specs=[pl.BlockSpec((tq, gw), lambda h, i: (i, h)),
                  pl.BlockSpec((t, gw), lambda h, i: (0, ng + h)),
                  pl.BlockSpec((t, gw), lambda h, i: (0, 2 * ng + h)),
                  pl.BlockSpec((2 * tq, tq), lambda h, i: (0, 0)),
                  pl.BlockSpec((1, gw), lambda h, i: (0, h))],
        out_specs=pl.BlockSpec((tq, gw), lambda h, i: (i, h)),
        out_shape=jax.ShapeDtypeStruct((t, n_heads * HEAD_DIM), BF16),
        scratch_shapes=[pltpu.VMEM((2, n_group, tq, tq), F32),
                        pltpu.VMEM((2, n_group, tq, tq), F32),
                        pltpu.VMEM((n_group, tq, HEAD_DIM), F32),
                        pltpu.VMEM((n_group, tq, HEAD_DIM), F32),
                        pltpu.VMEM((n_group, tq, HEAD_DIM), F32)],
        compiler_params=_params("arbitrary", "arbitrary"),
        name="sb_attention",
    )(qkv, qkv, qkv, u2, norm_w.reshape(1, n_heads * HEAD_DIM))


def _lb_kernel(p_ref, lb_ref):
    p = p_ref[...]
    depth = p.shape[0]
    e = jnp.exp(p - jnp.max(p, axis=0, keepdims=True))
    sm = e / jnp.sum(e, axis=0, keepdims=True)
    c = sm[0:1]
    first = c
    rows = [c - first]
    for l in range(1, depth):
        c = c + sm[l:l + 1]
        rows.append(c - first)
    lb_ref[...] = jnp.concatenate(rows, axis=0)


def hgrn_lower_bounds(p):
    return pl.pallas_call(
        _lb_kernel,
        out_shape=jax.ShapeDtypeStruct(p.shape, F32),
        name="hgrn_lower_bounds",
    )(p)


def _hgrn_kernel(q_ref, f_ref, i_ref, g_ref, lb_ref, nw_ref, o_ref, st_ref, *, n_heads):
    @pl.when(pl.program_id(0) == 0)
    def _():
        st_ref[...] = jnp.zeros_like(st_ref)

    for c in range(HG_CHUNKS_PER_STEP):
        _hgrn_chunk(slice(c * HG_CHUNK, (c + 1) * HG_CHUNK), q_ref, f_ref, i_ref, g_ref, lb_ref,
                    nw_ref, o_ref, st_ref, n_heads)


def _hgrn_chunk(rs, q_ref, f_ref, i_ref, g_ref, lb_ref, nw_ref, o_ref, st_ref, n_heads):
    dh = HEAD_DIM
    cc = HG_CHUNK
    nsub = cc // HG_SUB
    fz = f_ref[rs, :]
    lb = lb_ref[...]
    e = jnp.exp(-jnp.abs(fz))
    log_sig = jnp.minimum(fz, 0.0) - jnp.log1p(e)
    r = 1.0 / (1.0 + e)
    sig_neg = jnp.where(fz >= 0.0, e * r, r)
    a = jnp.log(lb)
    b2 = jnp.log1p(-lb) + log_sig
    log_f = jnp.maximum(a, b2) + jnp.log1p(jnp.exp(-jnp.abs(a - b2)))
    kk = (1.0 - lb) * sig_neg
    qz = q_ref[rs, :]
    qq = qz * _sigmoid(qz)
    gz = g_ref[rs, :]
    gate = gz * _sigmoid(gz)

    row_w = lax.broadcasted_iota(jnp.int32, log_f.shape, 0)
    b = log_f
    s = 1
    while s < cc:
        b = b + jnp.where(row_w >= s, pltpu.roll(b, s, axis=0), 0.0)
        s *= 2

    r2 = lax.broadcasted_iota(jnp.int32, (cc, cc), 0)
    c2 = lax.broadcasted_iota(jnp.int32, (cc, cc), 1)
    causal = c2 <= r2
    zeros_sub = jnp.zeros((HG_SUB, dh), BF16)

    for h in range(n_heads):
        sl = slice(h * dh, (h + 1) * dh)
        bh, qh, kh, vh = b[:, sl], qq[:, sl], kk[:, sl], i_ref[rs, sl]
        vb = vh.astype(BF16)
        q_parts, k_rows = [], []
        for jb in range(nsub):
            lo = jb * HG_SUB
            anchor = bh[lo:lo + 1]
            qd = (qh[lo:] * jnp.exp(bh[lo:] - anchor)).astype(BF16)
            q_parts.append(jnp.concatenate([jnp.zeros((lo, dh), BF16), qd], axis=0) if lo else qd)
            rows = slice(lo, lo + HG_SUB)
            ks = kh[rows] * jnp.exp(jnp.minimum(anchor - bh[rows], HG_EXP_CLAMP))
            k_rows.append(jnp.concatenate(
                [zeros_sub] * jb + [ks.astype(BF16)] + [zeros_sub] * (nsub - 1 - jb), axis=1))
        q_cat = jnp.concatenate(q_parts, axis=1)
        k_cat = jnp.concatenate(k_rows, axis=0)
        scores = lax.dot_general(q_cat, k_cat, (((1,), (1,)), ((), ())),
                                 preferred_element_type=F32)
        p = jnp.where(causal, scores, 0.0).astype(BF16)
        o = jnp.dot(p, vb, preferred_element_type=F32)
        st = st_ref[h]
        q0 = (qh * jnp.exp(bh)).astype(BF16)
        o = o + lax.dot_general(q0, st.astype(BF16), (((1,), (1,)), ((), ())),
                                preferred_element_type=F32)
        b_last = bh[cc - 1:cc]
        k_end = (kh * jnp.exp(b_last - bh)).astype(BF16)
        st_ref[h] = st * jnp.exp(b_last) + lax.dot_general(
            vb, k_end, (((0,), (0,)), ((), ())), preferred_element_type=F32)
        o = o * lax.rsqrt(jnp.mean(o * o, axis=-1, keepdims=True) + EPS)
        o_ref[rs, sl] = (o * nw_ref[:, sl] * gate[:, sl]).astype(o_ref.dtype)


def hgrn2(hg, lb, norm_w, n_heads):
    t = hg.shape[0]
    w = n_heads * HEAD_DIM
    cc = HG_CHUNK * HG_CHUNKS_PER_STEP
    assert t % cc == 0
    col = lambda c: pl.BlockSpec((cc, w), lambda i: (i, c))
    vec = pl.BlockSpec((1, w), lambda i: (0, 0))
    return pl.pallas_call(
        functools.partial(_hgrn_kernel, n_heads=n_heads),
        grid=(t // cc,),
        in_specs=[col(0), col(1), col(2), col(3), vec, vec],
        out_specs=pl.BlockSpec((cc, w), lambda i: (i, 0)),
        out_shape=jax.ShapeDtypeStruct((t, w), BF16),
        scratch_shapes=[pltpu.VMEM((n_heads, HEAD_DIM, HEAD_DIM), F32)],
        compiler_params=_params("arbitrary"),
        name="hgrn2",
    )(hg, hg, hg, hg, lb.reshape(1, w), norm_w.reshape(1, w))


def _up_kernel(a_ref, wg_ref, wu_ref, cwg_ref, cwu_ref, cbg_ref, cbu_ref, o_ref,
               wg_bf, wu_bf, tail_ref, *, tm, sub):
    i = pl.program_id(1)
    tn = o_ref.shape[1]
    sl = SUBLANES

    @pl.when(i == 0)
    def _():
        wg_bf[...] = wg_ref[...].astype(BF16)
        wu_bf[...] = wu_ref[...].astype(BF16)
        tail_ref[...] = jnp.zeros_like(tail_ref)

    srow = lax.broadcasted_iota(jnp.int32, (sub // sl, sl, tn), 1)

    def conv(x, tail, cw_ref, cb_ref):
        x3 = x.reshape(sub // sl, sl, tn)
        both = jnp.concatenate([tail[None], x3], axis=0)
        r1 = pltpu.roll(both, 1, axis=1)
        r2 = pltpu.roll(both, 2, axis=1)
        x1 = jnp.where(srow < 1, r1[:-1], r1[1:]).reshape(sub, tn)
        x2 = jnp.where(srow < 2, r2[:-1], r2[1:]).reshape(sub, tn)
        y = cb_ref[...] + cw_ref[0:1] * x2
        y = y + cw_ref[1:2] * x1
        y = y + cw_ref[2:3] * x
        return y, x3[-1]

    tg, tu = tail_ref[0], tail_ref[1]
    for r in range(tm // sub):
        a = a_ref[r * sub:(r + 1) * sub, :]
        g = jnp.dot(a, wg_bf[...], preferred_element_type=F32)
        u = jnp.dot(a, wu_bf[...], preferred_element_type=F32)
        gc, tg = conv(g, tg, cwg_ref, cbg_ref)
        uc, tu = conv(u, tu, cwu_ref, cbu_ref)
        o_ref[r * sub:(r + 1) * sub, :] = (gc * _sigmoid(gc) * uc).astype(o_ref.dtype)
    tail_ref[0] = tg
    tail_ref[1] = tu


def conv_gated_up(a, w_up, layer, conv_w, conv_b, *, tm, tn, sub=256):
    m, k = a.shape
    d_ff = w_up.shape[2] // 2
    assert d_ff % tn == 0 and m % tm == 0 and tm % sub == 0 and sub % SUBLANES == 0
    assert CONV_WIDTH - 1 <= SUBLANES
    nj = d_ff // tn
    cb = conv_b.reshape(1, 2 * d_ff)
    return pl.pallas_call(
        functools.partial(_up_kernel, tm=tm, sub=sub),
        grid=(nj, m // tm),
        in_specs=[pl.BlockSpec((tm, k), lambda j, i: (i, 0)),
                  pl.BlockSpec((None, k, tn), lambda j, i: (layer, 0, j)),
                  pl.BlockSpec((None, k, tn), lambda j, i: (layer, 0, j + nj)),
                  pl.BlockSpec((CONV_WIDTH, tn), lambda j, i: (0, j)),
                  pl.BlockSpec((CONV_WIDTH, tn), lambda j, i: (0, j + nj)),
                  pl.BlockSpec((1, tn), lambda j, i: (0, j)),
                  pl.BlockSpec((1, tn), lambda j, i: (0, j + nj))],
        out_specs=pl.BlockSpec((tm, tn), lambda j, i: (i, j)),
        out_shape=jax.ShapeDtypeStruct((m, d_ff), BF16),
        scratch_shapes=[pltpu.VMEM((k, tn), BF16), pltpu.VMEM((k, tn), BF16),
                        pltpu.VMEM((2, SUBLANES, tn), F32)],
        compiler_params=_params("arbitrary", "arbitrary"),
        name="conv_gated_up",
    )(a, w_up, w_up, conv_w, conv_w, cb, cb)


def kernel(x, norm1_w, w_in, sb_norm_w, hg_lb_param, hg_norm_w, w_out, norm2_w, w_up, conv_w,
           conv_b, w_down, final_norm_w):
    batch, seq, d_model = x.shape
    assert batch == 1
    depth = w_in.shape[0]
    sb_width = sb_norm_w.shape[1]
    hg_width = hg_norm_w.shape[1]
    sb_heads = sb_width // HEAD_DIM
    hg_heads = hg_width // HEAD_DIM
    scale = HEAD_DIM ** -0.5 * LOG2E

    lb_all = hgrn_lower_bounds(hg_lb_param)
    w_down_bf = w_down.astype(BF16)
    h = x.reshape(seq, d_model)
    for l in range(depth):
        hn = rmsnorm(h, norm1_w[l], BF16)
        qkv = matmul([hn], w_in, l, n_out=3 * sb_width, out_dtype=BF16, tm=1024, tn=512,
                     scale=scale, n_scaled_cols=sb_width)
        hg = matmul([hn], w_in, l, n_out=4 * hg_width, col_off=3 * sb_width, tm=1024, tn=512)
        o_sb = sb_attention(qkv, sb_norm_w[l], sb_heads)
        o_hg = hgrn2(hg, lb_all[l], hg_norm_w[l], hg_heads)
        h = matmul([o_sb, o_hg], w_out, l, n_out=d_model, res=h, tm=1024, tn=512)
        hn = rmsnorm(h, norm2_w[l], BF16)
        act = conv_gated_up(hn, w_up, l, conv_w[l], conv_b[l], tm=1024, tn=256)
        h = matmul([act], w_down_bf, l, n_out=d_model, res=h, tm=512, tn=512)
    return rmsnorm(h, final_norm_w, F32).reshape(batch, seq, d_model)
```
